```python
import jax, jax.numpy as jnp
from jax import lax
import numpy as np

D_MODEL = 1024
BATCH = 32
SEQ = 2048
DEPTH = 4
DEC_BATCH = 8
DEC_SEQ = 4096
PAST_LEN = 128

GRID_W = 64
HEAD_DIM = 64
A_HEADS = 8
A_KV_HEADS = 2
A_GROUP = A_HEADS // A_KV_HEADS
B_HEADS = 8
NA_ROWS_MAX = 8
NA_COLS = 16
Q_BLOCK = 128
D_FF = 4 * D_MODEL
ROPE_THETA = 10000.0
ALPHA = (2 * DEPTH) ** 0.25
BETA = (8 * DEPTH) ** -0.25
EPS = 1e-6
A_Q = A_HEADS * HEAD_DIM
A_KV = A_KV_HEADS * HEAD_DIM
B_W = B_HEADS * HEAD_DIM
N_IN = A_Q + 2 * A_KV + 3 * B_W + 2 * D_MODEL
SPLIT_POINTS = [A_Q, A_Q + A_KV, A_Q + 2 * A_KV, A_Q + 2 * A_KV + B_W,
                A_Q + 2 * A_KV + 2 * B_W, A_Q + 2 * A_KV + 3 * B_W,
                A_Q + 2 * A_KV + 3 * B_W + D_MODEL]
RPB_R = 2 * NA_ROWS_MAX - 1
RPB_C = 2 * NA_COLS - 1

kernel_name = "hybrid_gqa_natten_deepnorm_encoder"


def _layer_norm(x):
    xf = x.astype(jnp.float32)
    mu = jnp.mean(xf, -1, keepdims=True)
    var = jnp.mean(jnp.square(xf - mu), -1, keepdims=True)
    return ((xf - mu) * lax.rsqrt(var + EPS)).astype(x.dtype)


def _layer_norm_affine(x, g, b):
    xf = x.astype(jnp.float32)
    mu = jnp.mean(xf, -1, keepdims=True)
    var = jnp.mean(jnp.square(xf - mu), -1, keepdims=True)
    y = (xf - mu) * lax.rsqrt(var + EPS) * g.astype(jnp.float32) + b.astype(jnp.float32)
    return y.astype(x.dtype)


def _rms_norm(x, g):
    xf = x.astype(jnp.float32)
    y = xf * lax.rsqrt(jnp.mean(jnp.square(xf), -1, keepdims=True) + EPS) * g.astype(jnp.float32)
    return y.astype(x.dtype)


def _axial_rope_tables(n_tok):
    t = np.arange(n_tok)
    n_pairs_axis = HEAD_DIM // 4
    inv_freq = 1.0 / (ROPE_THETA ** (np.arange(n_pairs_axis) * 2.0 / (HEAD_DIM // 2)))
    ang = np.concatenate([(t // GRID_W)[:, None] * inv_freq[None, :],
                          (t % GRID_W)[:, None] * inv_freq[None, :]], -1)
    return jnp.asarray(np.cos(ang), jnp.float32), jnp.asarray(np.sin(ang), jnp.float32)


def _apply_rope(x, cos, sin):
    xf = x.astype(jnp.float32).reshape(x.shape[:-1] + (HEAD_DIM // 2, 2))
    x0, x1 = xf[..., 0], xf[..., 1]
    c = cos[None, :, None, :]
    s = sin[None, :, None, :]
    out = jnp.stack([x0 * c - x1 * s, x0 * s + x1 * c], -1).reshape(x.shape)
    return out.astype(x.dtype)


def _gqa_attention(q, k, v):
    bn, s_len = q.shape[0], q.shape[1]
    nblk = s_len // Q_BLOCK
    qb = q.reshape(bn, nblk, Q_BLOCK, A_KV_HEADS, A_GROUP, HEAD_DIM).transpose(1, 0, 2, 3, 4, 5)
    scale = HEAD_DIM ** -0.5

    def one_block(qi):
        s = jnp.einsum('bqkgd,bskd->bkgqs', qi, k).astype(jnp.float32) * scale
        p = jax.nn.softmax(s, axis=-1).astype(v.dtype)
        return jnp.einsum('bkgqs,bskd->bqkgd', p, v)

    o = lax.map(one_block, qb)
    return o.transpose(1, 0, 2, 3, 4, 5).reshape(bn, s_len, A_Q)


def _neighbourhood_tables(n_tok):
    rows = n_tok // GRID_W
    kr = min(NA_ROWS_MAX, rows)
    t = np.arange(n_tok)
    r, c = t // GRID_W, t % GRID_W
    rs = np.clip(r - kr // 2, 0, rows - kr)
    cs = np.clip(c - NA_COLS // 2, 0, GRID_W - NA_COLS)
    kr_idx = rs[:, None] + np.arange(kr)[None, :]
    kc_idx = cs[:, None] + np.arange(NA_COLS)[None, :]
    key_idx = (kr_idx[:, :, None] * GRID_W + kc_idx[:, None, :]).reshape(n_tok, -1)
    dr = kr_idx - r[:, None] + (NA_ROWS_MAX - 1)
    dc = kc_idx - c[:, None] + (NA_COLS - 1)
    bias_idx = (dr[:, :, None] * RPB_C + dc[:, None, :]).reshape(n_tok, -1)
    return key_idx.astype(np.int32), bias_idx.astype(np.int32)


def _neighbourhood_attention(q, k, v, rpb):
    bn, s_len = q.shape[0], q.shape[1]
    key_idx, bias_idx = _neighbourhood_tables(s_len)
    nblk = s_len // GRID_W
    n_win = key_idx.shape[1]
    qb = q.reshape(bn, nblk, GRID_W, B_HEADS, HEAD_DIM).transpose(1, 0, 2, 3, 4)
    kidx = jnp.asarray(key_idx.reshape(nblk, GRID_W, n_win))
    bias = rpb.reshape(B_HEADS, -1)[:, jnp.asarray(bias_idx)].astype(jnp.float32)
    bias = bias.reshape(B_HEADS, nblk, GRID_W, n_win).transpose(1, 0, 2, 3)
    scale = HEAD_DIM ** -0.5

    def one_block(args):
        qi, ki, bi = args
        kg = k[:, ki]
        vg = v[:, ki]
        s = jnp.einsum('bqhd,bqwhd->bhqw', qi, kg).astype(jnp.float32) * scale + bi[None]
        p = jax.nn.softmax(s, axis=-1).astype(v.dtype)
        return jnp.einsum('bhqw,bqwhd->bqhd', p, vg)

    o = lax.map(one_block, (qb, kidx, bias))
    return o.transpose(1, 0, 2, 3, 4).reshape(bn, s_len, B_W)


def _encoder_layer(x, c, cos, sin, w_ada, b_ada, w_in, q_norm_a, k_norm_a, rpb_b,
                   w_out_a, w_out_b, w_out, ln_mix_g, ln_mix_b,
                   w_ff1, w_ff2, ln_ff_g, ln_ff_b):
    bn, s_len, _ = x.shape
    mod = jnp.dot(jax.nn.silu(c), w_ada) + b_ada
    sh1, sc1, g1, sh2, sc2, g2 = jnp.split(mod[:, None, :], 6, axis=-1)

    u = _layer_norm(x) * (1.0 + sc1) + sh1
    proj = jnp.dot(u, w_in)
    qa, ka, va, qb, kb, vb, ga, gb = jnp.split(proj, SPLIT_POINTS, axis=-1)
    qa = _apply_rope(_rms_norm(qa.reshape(bn, s_len, A_HEADS, HEAD_DIM), q_norm_a), cos, sin)
    ka = _apply_rope(_rms_norm(ka.reshape(bn, s_len, A_KV_HEADS, HEAD_DIM), k_norm_a), cos, sin)
    va = va.reshape(bn, s_len, A_KV_HEADS, HEAD_DIM)
    oa = jnp.dot(_gqa_attention(qa, ka, va), w_out_a)
    ob = jnp.dot(_neighbourhood_attention(
        qb.reshape(bn, s_len, B_HEADS, HEAD_DIM),
        kb.reshape(bn, s_len, B_HEADS, HEAD_DIM),
        vb.reshape(bn, s_len, B_HEADS, HEAD_DIM), rpb_b), w_out_b)
    mixed = jnp.dot(jax.nn.sigmoid(ga) * oa + jax.nn.sigmoid(gb) * ob, w_out)
    x = _layer_norm_affine(ALPHA * x + (1.0 + g1) * mixed, ln_mix_g, ln_mix_b)

    u = _layer_norm(x) * (1.0 + sc2) + sh2
    h = jnp.dot(jnp.square(jax.nn.relu(jnp.dot(u, w_ff1))), w_ff2)
    x = _layer_norm_affine(ALPHA * x + (1.0 + g2) * h, ln_ff_g, ln_ff_b)
    return x


def setup_inputs(seed: int = 0) -> dict:
    key = jax.random.key(seed)
    ks = jax.random.split(key, 20)
    f32 = jnp.float32
    nrm = lambda k, shape, s: jax.random.normal(k, shape, f32) * s
    return {
        "x_prompt": nrm(ks[0], (BATCH, SEQ, D_MODEL), 1.0),
        "x_sample": nrm(ks[1], (DEC_BATCH, DEC_SEQ, D_MODEL), 1.0),
        "c_prompt": nrm(ks[2], (BATCH, D_MODEL), 1.0),
        "c_sample": nrm(ks[3], (DEC_BATCH, D_MODEL), 1.0),
        "w_ada": nrm(ks[4], (DEPTH, D_MODEL, 6 * D_MODEL), 0.5 * D_MODEL ** -0.5),
        "b_ada": nrm(ks[5], (DEPTH, 6 * D_MODEL), 0.02),
        "w_in": nrm(ks[6], (DEPTH, D_MODEL, N_IN), D_MODEL ** -0.5),
        "q_norm_a": 1.0 + nrm(ks[7], (DEPTH, HEAD_DIM), 0.05),
        "k_norm_a": 1.0 + nrm(ks[8], (DEPTH, HEAD_DIM), 0.05),
        "rpb_b": nrm(ks[9], (DEPTH, B_HEADS, RPB_R, RPB_C), 0.1),
        "w_out_a": nrm(ks[10], (DEPTH, A_Q, D_MODEL), A_Q ** -0.5),
        "w_out_b": nrm(ks[11], (DEPTH, B_W, D_MODEL), B_W ** -0.5),
        "w_out": nrm(ks[12], (DEPTH, D_MODEL, D_MODEL), BETA * D_MODEL ** -0.5),
        "ln_mix_g": 1.0 + nrm(ks[13], (DEPTH, D_MODEL), 0.05),
        "ln_mix_b": nrm(ks[14], (DEPTH, D_MODEL), 0.02),
        "w_ff1": nrm(ks[15], (DEPTH, D_MODEL, D_FF), D_MODEL ** -0.5),
        "w_ff2": nrm(ks[16], (DEPTH, D_FF, D_MODEL), BETA * D_FF ** -0.5),
        "ln_ff_g": 1.0 + nrm(ks[17], (DEPTH, D_MODEL), 0.05),
        "ln_ff_b": nrm(ks[18], (DEPTH, D_MODEL), 0.02),
    }


def reference(x_prompt, x_sample, c_prompt, c_sample, w_ada, b_ada, w_in, q_norm_a, k_norm_a,
              rpb_b, w_out_a, w_out_b, w_out, ln_mix_g, ln_mix_b, w_ff1, w_ff2, ln_ff_g, ln_ff_b):
    cos_p, sin_p = _axial_rope_tables(x_prompt.shape[1])
    cos_s, sin_s = _axial_rope_tables(x_sample.shape[1])
    y_prompt = x_prompt
    y_sample = x_sample
    for l in range(DEPTH):
        layer_params = (w_ada[l], b_ada[l], w_in[l], q_norm_a[l], k_norm_a[l], rpb_b[l],
                        w_out_a[l], w_out_b[l], w_out[l], ln_mix_g[l], ln_mix_b[l],
                        w_ff1[l], w_ff2[l], ln_ff_g[l], ln_ff_b[l])
        y_prompt = _encoder_layer(y_prompt, c_prompt, cos_p, sin_p, *layer_params)
        y_sample = _encoder_layer(y_sample, c_sample, cos_s, sin_s, *layer_params)
    return (y_prompt, y_sample)
```

```python
import functools

import numpy as np
import jax
import jax.numpy as jnp
from jax import lax
from jax.experimental import pallas as pl
from jax.experimental.pallas import tpu as pltpu

F32 = jnp.float32
BF16 = jnp.bfloat16

D_MODEL = 1024
DEPTH = 4
GRID_W = 64
HEAD_DIM = 64
A_HEADS = 8
A_KV_HEADS = 2
A_GROUP = A_HEADS // A_KV_HEADS
B_HEADS = 8
NA_ROWS = 8
NA_COLS = 16
D_FF = 4 * D_MODEL
ROPE_THETA = 10000.0
ALPHA = (2 * DEPTH) ** 0.25
EPS = 1e-6
A_Q = A_HEADS * HEAD_DIM
A_KV = A_KV_HEADS * HEAD_DIM
B_W = B_HEADS * HEAD_DIM
N_QKV = A_Q + 2 * A_KV + 3 * B_W
RPB_C = 2 * NA_COLS - 1
SCALE = HEAD_DIM ** -0.5

LANES = 128
NA_Q_ROWS = 4
NA_K_ROWS = 12
NA_Q = NA_Q_ROWS * GRID_W
NA_K = NA_K_ROWS * GRID_W
MASKED = -1e30
VMEM_LIMIT = 56 * 1024 * 1024


def _const_spec(shape):
    nd = len(shape)
    return pl.BlockSpec(shape, lambda *_: (0,) * nd, pipeline_mode=pl.Buffered(1))


def _params(n_axes):
    return pltpu.CompilerParams(dimension_semantics=("arbitrary",) * n_axes,
                                vmem_limit_bytes=VMEM_LIMIT)


def _layer_norm(x):
    mu = jnp.mean(x, -1, keepdims=True)
    xc = x - mu
    var = jnp.mean(xc * xc, -1, keepdims=True)
    return xc * lax.rsqrt(var + EPS)


def _dot(a, b):
    return jnp.dot(a, b, preferred_element_type=F32)


def _dot_nt(a, b):
    return lax.dot_general(a, b, (((1,), (1,)), ((), ())), preferred_element_type=F32)


def _mod_kernel(c_ref, w_ref, b_ref, o_ref):
    c = c_ref[...]
    a = (c * jax.nn.sigmoid(c)).astype(BF16)
    o_ref[0] = _dot(a, w_ref[0].astype(BF16)) + b_ref[0]


def _modulation(c_all, w_ada, b_ada):
    n = c_all.shape[0]
    tn = 1536
    return pl.pallas_call(
        _mod_kernel,
        grid=(DEPTH, 6 * D_MODEL // tn),
        in_specs=[pl.BlockSpec((n, D_MODEL), lambda l, j: (0, 0)),
                  pl.BlockSpec((1, D_MODEL, tn), lambda l, j: (l, 0, j)),
                  pl.BlockSpec((1, 1, tn), lambda l, j: (l, 0, j))],
        out_specs=pl.BlockSpec((1, n, tn), lambda l, j: (l, 0, j)),
        out_shape=jax.ShapeDtypeStruct((DEPTH, n, 6 * D_MODEL), F32),
        compiler_params=_params(2),
        name="adaln_modulation",
    )(c_all, w_ada, b_ada.reshape(DEPTH, 1, 6 * D_MODEL))


def _head_rms_rope(xh, e, a, b):
    x2 = xh * xh
    hi = x2.astype(BF16)
    lo = (x2 - hi.astype(F32)).astype(BF16)
    ms = _dot(hi, e) + _dot(lo, e)
    n = xh.shape[-1]
    lane = lax.broadcasted_iota(jnp.int32, xh.shape, 1)
    partner = jnp.where(lane % 2 == 0, pltpu.roll(xh, n - 1, 1), pltpu.roll(xh, 1, 1))
    return (xh * a + partner * b) * lax.rsqrt(ms + EPS)


def _pre_kernel(x_ref, mod_ref, w_ref, e_ref, aq_ref, bq_ref, ak_ref, bk_ref,
                qa_ref, ka_ref, va_ref, qb_ref, kb_ref, vb_ref):
    x = x_ref[0]
    u = _layer_norm(x) * (1.0 + mod_ref[0, 1:2, :]) + mod_ref[0, 0:1, :]
    ub = u.astype(BF16)
    o = 0
    qa = _dot(ub, w_ref[:, o:o + A_Q]); o += A_Q
    qa_ref[0] = _head_rms_rope(qa, e_ref[...], aq_ref[...], bq_ref[...]).astype(BF16)
    ka = _dot(ub, w_ref[:, o:o + A_KV]); o += A_KV
    ka_ref[0] = _head_rms_rope(ka, e_ref[:A_KV, :A_KV], ak_ref[...], bk_ref[...]).astype(BF16)
    va_ref[0] = _dot(ub, w_ref[:, o:o + A_KV]).astype(BF16); o += A_KV
    qb_ref[0] = (_dot(ub, w_ref[:, o:o + B_W]) * SCALE).astype(BF16); o += B_W
    kb_ref[0] = _dot(ub, w_ref[:, o:o + B_W]).astype(BF16); o += B_W
    vb_ref[0] = _dot(ub, w_ref[:, o:o + B_W]).astype(BF16)


def _pre_attention(x, mod, w_qkv, e, aq, bq, ak, bk, tile):
    bn, s, _ = x.shape
    tok = lambda width: pl.BlockSpec((1, tile, width), lambda b, i: (b, i, 0))
    tab = lambda width: pl.BlockSpec((tile, width), lambda b, i: (i, 0))
    out = lambda width: jax.ShapeDtypeStruct((bn, s, width), BF16)
    return pl.pallas_call(
        _pre_kernel,
        grid=(bn, s // tile),
        in_specs=[tok(D_MODEL),
                  pl.BlockSpec((1, 6, D_MODEL), lambda b, i: (b, 0, 0)),
                  _const_spec((D_MODEL, N_QKV)),
                  _const_spec((A_Q, A_Q)),
                  tab(A_Q), tab(A_Q), tab(A_KV), tab(A_KV)],
        out_specs=[tok(A_Q), tok(A_KV), tok(A_KV), tok(B_W), tok(B_W), tok(B_W)],
        out_shape=[out(A_Q), out(A_KV), out(A_KV), out(B_W), out(B_W), out(B_W)],
        compiler_params=_params(2),
        name="pre_attention",
    )(x, mod, w_qkv, e, aq, bq, ak, bk)


def _softmax_pv(s, v):
    m = jnp.max(s, -1, keepdims=True)
    p = jnp.exp(s - m)
    l = jnp.sum(p, -1, keepdims=True)
    return _dot(p.astype(BF16), v) / l


def _gqa_kernel(q_ref, k_ref, v_ref, o_ref):
    q = q_ref[0]
    k = k_ref[0]
    v = v_ref[0]
    left = lax.broadcasted_iota(jnp.int32, q.shape, 1) < HEAD_DIM
    zero = jnp.zeros_like(q)
    o_left = _softmax_pv(_dot_nt(jnp.where(left, q, zero), k), v)
    o_right = _softmax_pv(_dot_nt(jnp.where(left, zero, q), k), v)
    o_ref[0] = jnp.where(left, o_left, o_right).astype(BF16)


def _gqa_attention(qa, ka, va, tile):
    bn, s, _ = qa.shape
    n_pair = A_Q // LANES
    return pl.pallas_call(
        _gqa_kernel,
        grid=(bn, s // tile, n_pair),
        in_specs=[pl.BlockSpec((1, tile, LANES), lambda b, i, c: (b, i, c)),
                  pl.BlockSpec((1, s, A_KV), lambda b, i, c: (b, 0, 0)),
                  pl.BlockSpec((1, s, A_KV), lambda b, i, c: (b, 0, 0))],
        out_specs=pl.BlockSpec((1, tile, LANES), lambda b, i, c: (b, i, c)),
        out_shape=jax.ShapeDtypeStruct((bn, s, A_Q), BF16),
        compiler_params=_params(3),
        name="gqa_attention",
    )(qa, ka, va)


def _na_key_start(rb, n_rows):
    return jnp.clip(rb * NA_Q_ROWS - NA_ROWS // 2, 0, n_rows - NA_K_ROWS)


def _na_kernel(q_ref, k_ref, v_ref, bias_ref, o_ref, *, n_rows):
    start = pl.multiple_of(_na_key_start(pl.program_id(2), n_rows) * GRID_W, GRID_W)
    q = q_ref[0]
    k = k_ref[0, pl.ds(start, NA_K), :]
    v = v_ref[0, pl.ds(start, NA_K), :]
    left = lax.broadcasted_iota(jnp.int32, q.shape, 1) < HEAD_DIM
    zero = jnp.zeros_like(q)
    o_left = _softmax_pv(_dot_nt(jnp.where(left, q, zero), k) + bias_ref[0, 0], v)
    o_right = _softmax_pv(_dot_nt(jnp.where(left, zero, q), k) + bias_ref[0, 1], v)
    o_ref[0] = jnp.where(left, o_left, o_right).astype(BF16)


def _na_bias_index():
    n_rows = 3 * NA_K_ROWS
    idx = np.zeros((3, NA_Q, NA_K), np.int32)
    valid = np.zeros((3, NA_Q, NA_K), bool)
    for kind, r0 in enumerate((0, NA_K_ROWS, n_rows - NA_Q_ROWS)):
        ws = int(np.clip(r0 - NA_ROWS // 2, 0, n_rows - NA_K_ROWS))
        qi = np.arange(NA_Q)
        r, c = r0 + qi // GRID_W, qi % GRID_W
        rs = np.clip(r - NA_ROWS // 2, 0, n_rows - NA_ROWS)
        cs = np.clip(c - NA_COLS // 2, 0, GRID_W - NA_COLS)
        kj = np.arange(NA_K)
        kr, kc = ws + kj // GRID_W, kj % GRID_W
        ok = ((kr[None, :] >= rs[:, None]) & (kr[None, :] < rs[:, None] + NA_ROWS)
              & (kc[None, :] >= cs[:, None]) & (kc[None, :] < cs[:, None] + NA_COLS))
        dr = kr[None, :] - r[:, None] + (NA_ROWS - 1)
        dc = kc[None, :] - c[:, None] + (NA_COLS - 1)
        idx[kind] = np.where(ok, dr * RPB_C + dc, 0)
        valid[kind] = ok
    return idx, valid


def _na_bias_table(rpb):
    idx, valid = _na_bias_index()
    flat = rpb.reshape(B_HEADS, -1)
    table = jnp.where(valid[None], flat[:, idx], MASKED)
    return jnp.transpose(table, (1, 0, 2, 3))


def _neighbourhood_attention(qb, kb, vb, bias):
    bn, s, _ = qb.shape
    n_rows = s // GRID_W
    assert n_rows >= NA_K_ROWS + NA_Q_ROWS and n_rows % NA_Q_ROWS == 0
    n_blk = n_rows // NA_Q_ROWS
    n_pair = B_W // LANES

    def kind(rb):
        return jnp.where(rb == 0, 0, jnp.where(rb == n_blk - 1, 2, 1))

    return pl.pallas_call(
        functools.partial(_na_kernel, n_rows=n_rows),
        grid=(bn, n_pair, n_blk),
        in_specs=[pl.BlockSpec((1, NA_Q, LANES), lambda b, c, r: (b, r, c)),
                  pl.BlockSpec((1, s, LANES), lambda b, c, r: (b, 0, c)),
                  pl.BlockSpec((1, s, LANES), lambda b, c, r: (b, 0, c)),
                  pl.BlockSpec((1, 2, NA_Q, NA_K), lambda b, c, r: (kind(r), c, 0, 0))],
        out_specs=pl.BlockSpec((1, NA_Q, LANES), lambda b, c, r: (b, r, c)),
        out_shape=jax.ShapeDtypeStruct((bn, s, B_W), BF16),
        compiler_params=_params(3),
        name="neighbourhood_attention",
    )(qb, kb, vb, bias)


def _mix_kernel(x_ref, mod_ref, oa_ref, ob_ref, wg_ref, woa_ref, wob_ref, wo_ref, g_ref, b_ref,
                y_ref):
    x = x_ref[0]
    u = _layer_norm(x) * (1.0 + mod_ref[0, 1:2, :]) + mod_ref[0, 0:1, :]
    ub = u.astype(BF16)
    gate_a = jax.nn.sigmoid(_dot(ub, wg_ref[:, :D_MODEL]))
    gate_b = jax.nn.sigmoid(_dot(ub, wg_ref[:, D_MODEL:]))
    merged = gate_a * _dot(oa_ref[0], woa_ref[...]) + gate_b * _dot(ob_ref[0], wob_ref[...])
    mixed = _dot(merged.astype(BF16), wo_ref[...])
    y = ALPHA * x + (1.0 + mod_ref[0, 2:3, :]) * mixed
    y_ref[0] = _layer_norm(y) * g_ref[...] + b_ref[...]


def _mix(x, mod, oa, ob, w_gate, w_out_a, w_out_b, w_out, ln_g, ln_b, tile):
    bn, s, _ = x.shape
    tok = lambda width: pl.BlockSpec((1, tile, width), lambda b, i: (b, i, 0))
    return pl.pallas_call(
        _mix_kernel,
        grid=(bn, s // tile),
        in_specs=[tok(D_MODEL),
                  pl.BlockSpec((1, 6, D_MODEL), lambda b, i: (b, 0, 0)),
                  tok(A_Q), tok(B_W),
                  _const_spec((D_MODEL, 2 * D_MODEL)),
                  _const_spec((A_Q, D_MODEL)),
                  _const_spec((B_W, D_MODEL)),
                  _const_spec((D_MODEL, D_MODEL)),
                  _const_spec((1, D_MODEL)), _const_spec((1, D_MODEL))],
        out_specs=tok(D_MODEL),
        out_shape=jax.ShapeDtypeStruct(x.shape, F32),
        compiler_params=_params(2),
        name="mix",
    )(x, mod, oa, ob, w_gate, w_out_a, w_out_b, w_out, ln_g, ln_b)


def _ffn_kernel(x_ref, mod_ref, w1_ref, w2_ref, g_ref, b_ref, y_ref):
    x = x_ref[0]
    u = _layer_norm(x) * (1.0 + mod_ref[0, 4:5, :]) + mod_ref[0, 3:4, :]
    h = jnp.square(jnp.maximum(_dot(u.astype(BF16), w1_ref[...]), 0.0))
    f = _dot(h.astype(BF16), w2_ref[...])
    y = ALPHA * x + (1.0 + mod_ref[0, 5:6, :]) * f
    y_ref[0] = _layer_norm(y) * g_ref[...] + b_ref[...]


def _ffn(x, mod, w1, w2, ln_g, ln_b, tile):
    bn, s, _ = x.shape
    tok = pl.BlockSpec((1, tile, D_MODEL), lambda b, i: (b, i, 0))
    return pl.pallas_call(
        _ffn_kernel,
        grid=(bn, s // tile),
        in_specs=[tok,
                  pl.BlockSpec((1, 6, D_MODEL), lambda b, i: (b, 0, 0)),
                  _const_spec((D_MODEL, D_FF)),
                  _const_spec((D_FF, D_MODEL)),
                  _const_spec((1, D_MODEL)), _const_spec((1, D_MODEL))],
        out_specs=tok,
        out_shape=jax.ShapeDtypeStruct(x.shape, F32),
        compiler_params=_params(2),
        name="ffn",
    )(x, mod, w1, w2, ln_g, ln_b)


def _rope_tables(n_tok, gain):
    t = np.arange(n_tok)
    n_pairs_axis = HEAD_DIM // 4
    inv_freq = 1.0 / (ROPE_THETA ** (np.arange(n_pairs_axis) * 2.0 / (HEAD_DIM // 2)))
    ang = np.concatenate([(t // GRID_W)[:, None] * inv_freq[None, :],
                          (t % GRID_W)[:, None] * inv_freq[None, :]], -1)
    cos = np.repeat(np.cos(ang), 2, axis=-1).astype(np.float32)
    sin = np.repeat(np.sin(ang), 2, axis=-1).astype(np.float32)
    sin[:, 0::2] *= -1.0
    partner_gain = gain.reshape(HEAD_DIM // 2, 2)[:, ::-1].reshape(HEAD_DIM)
    return jnp.asarray(cos) * gain[None, :], jnp.asarray(sin) * partner_gain[None, :]


def _head_order():
    order = []
    for j in range(A_GROUP):
        order += [j, A_GROUP + j]
    return np.concatenate([np.arange(h * HEAD_DIM, (h + 1) * HEAD_DIM) for h in order])


def _layer_params(l, n_toks, w_in, q_norm_a, k_norm_a, rpb_b, w_out_a, w_out_b, w_out,
                  ln_mix_g, ln_mix_b, w_ff1, w_ff2, ln_ff_g, ln_ff_b):
    perm = _head_order()
    w = w_in[l]
    w_qkv = jnp.concatenate([w[:, :A_Q][:, perm], w[:, A_Q:N_QKV]], axis=1).astype(BF16)
    rope = {}
    for n_tok in n_toks:
        aq, bq = _rope_tables(n_tok, q_norm_a[l])
        ak, bk = _rope_tables(n_tok, k_norm_a[l])
        rope[n_tok] = (jnp.tile(aq, (1, A_HEADS)) * SCALE, jnp.tile(bq, (1, A_HEADS)) * SCALE,
                       jnp.tile(ak, (1, A_KV_HEADS)), jnp.tile(bk, (1, A_KV_HEADS)))
    row = lambda v: v[l].reshape(1, D_MODEL)
    return dict(
        w_qkv=w_qkv,
        rope=rope,
        bias=_na_bias_table(rpb_b[l]),
        w_gate=w[:, N_QKV:].astype(BF16),
        w_out_a=w_out_a[l][perm, :].astype(BF16),
        w_out_b=w_out_b[l].astype(BF16),
        w_out=w_out[l].astype(BF16),
        ln_mix=(row(ln_mix_g), row(ln_mix_b)),
        w_ff1=w_ff1[l].astype(BF16),
        w_ff2=w_ff2[l].astype(BF16),
        ln_ff=(row(ln_ff_g), row(ln_ff_b)),
    )


def _head_mean_matrix():
    blk = np.arange(A_Q) // HEAD_DIM
    return jnp.asarray((blk[:, None] == blk[None, :]) / HEAD_DIM, BF16)


def _encoder_layer(x, mod, p, e):
    s = x.shape[1]
    qa, ka, va, qb, kb, vb = _pre_attention(x, mod, p["w_qkv"], e, *p["rope"][s], tile=512)
    oa = _gqa_attention(qa, ka, va, tile=512)
    ob = _neighbourhood_attention(qb, kb, vb, p["bias"])
    x = _mix(x, mod, oa, ob, p["w_gate"], p["w_out_a"], p["w_out_b"], p["w_out"], *p["ln_mix"],
             tile=512)
    return _ffn(x, mod, p["w_ff1"], p["w_ff2"], *p["ln_ff"], tile=512)


def kernel(x_prompt, x_sample, c_prompt, c_sample, w_ada, b_ada, w_in, q_norm_a, k_norm_a, rpb_b,
           w_out_a, w_out_b, w_out, ln_mix_g, ln_mix_b, w_ff1, w_ff2, ln_ff_g, ln_ff_b):
    n_prompt, n_sample = c_prompt.shape[0], c_sample.shape[0]
    c_all = jnp.concatenate([c_prompt, c_sample], axis=0)
    c_all = jnp.pad(c_all, ((0, -c_all.shape[0] % 16), (0, 0)))
    mod = _modulation(c_all, w_ada, b_ada).reshape(DEPTH, -1, 6, D_MODEL)
    e = _head_mean_matrix()
    n_toks = (x_prompt.shape[1], x_sample.shape[1])
    y_prompt, y_sample = x_prompt, x_sample
    for l in range(DEPTH):
        p = _layer_params(l, n_toks, w_in, q_norm_a, k_norm_a, rpb_b, w_out_a, w_out_b, w_out,
                          ln_mix_g, ln_mix_b, w_ff1, w_ff2, ln_ff_g, ln_ff_b)
        y_prompt = _encoder_layer(y_prompt, mod[l, :n_prompt], p, e)
        y_sample = _encoder_layer(y_sample, mod[l, n_prompt:n_prompt + n_sample], p, e)
    return (y_prompt, y_sample)
```

```python
import functools

import numpy as np
import jax
import jax.numpy as jnp
from jax import lax
from jax.experimental import pallas as pl
from jax.experimental.pallas import tpu as pltpu

F32 = jnp.float32
BF16 = jnp.bfloat16

D_MODEL = 1024
DEPTH = 4
GRID_W = 64
HEAD_DIM = 64
A_HEADS = 8
A_KV_HEADS = 2
A_GROUP = A_HEADS // A_KV_HEADS
B_HEADS = 8
NA_ROWS = 8
NA_COLS = 16
D_FF = 4 * D_MODEL
ROPE_THETA = 10000.0
ALPHA = (2 * DEPTH) ** 0.25
EPS = 1e-6
A_Q = A_HEADS * HEAD_DIM
A_KV = A_KV_HEADS * HEAD_DIM
B_W = B_HEADS * HEAD_DIM
N_QKV = A_Q + 2 * A_KV + 3 * B_W
RPB_C = 2 * NA_COLS - 1
SCALE = HEAD_DIM ** -0.5

LANES = 128
NA_Q_ROWS = 4
NA_K_ROWS = 12
NA_Q = NA_Q_ROWS * GRID_W
NA_K = NA_K_ROWS * GRID_W
MASKED = -1e30
VMEM_LIMIT = 56 * 1024 * 1024


def _const_spec(shape):
    nd = len(shape)
    return pl.BlockSpec(shape, lambda *_: (0,) * nd, pipeline_mode=pl.Buffered(1))


def _params(n_axes):
    return pltpu.CompilerParams(dimension_semantics=("arbitrary",) * n_axes,
                                vmem_limit_bytes=VMEM_LIMIT)


def _layer_norm(x):
    mu = jnp.mean(x, -1, keepdims=True)
    xc = x - mu
    var = jnp.mean(xc * xc, -1, keepdims=True)
    return xc * lax.rsqrt(var + EPS)


def _dot(a, b):
    return jnp.dot(a, b, preferred_element_type=F32)


def _dot_nt(a, b):
    return lax.dot_general(a, b, (((1,), (1,)), ((), ())), preferred_element_type=F32)


def _mod_kernel(c_ref, w_ref, b_ref, o_ref):
    c = c_ref[...]
    a = (c * jax.nn.sigmoid(c)).astype(BF16)
    o_ref[0] = _dot(a, w_ref[0].astype(BF16)) + b_ref[0]


def _modulation(c_all, w_ada, b_ada):
    n = c_all.shape[0]
    tn = 1536
    return pl.pallas_call(
        _mod_kernel,
        grid=(DEPTH, 6 * D_MODEL // tn),
        in_specs=[pl.BlockSpec((n, D_MODEL), lambda l, j: (0, 0)),
                  pl.BlockSpec((1, D_MODEL, tn), lambda l, j: (l, 0, j)),
                  pl.BlockSpec((1, 1, tn), lambda l, j: (l, 0, j))],
        out_specs=pl.BlockSpec((1, n, tn), lambda l, j: (l, 0, j)),
        out_shape=jax.ShapeDtypeStruct((DEPTH, n, 6 * D_MODEL), F32),
        compiler_params=_params(2),
        name="adaln_modulation",
    )(c_all, w_ada, b_ada.reshape(DEPTH, 1, 6 * D_MODEL))


def _head_rms_rope(xh, e, a, b):
    x2 = xh * xh
    hi = x2.astype(BF16)
    lo = (x2 - hi.astype(F32)).astype(BF16)
    ms = _dot(hi, e) + _dot(lo, e)
    n = xh.shape[-1]
    lane = lax.broadcasted_iota(jnp.int32, xh.shape, 1)
    partner = jnp.where(lane % 2 == 0, pltpu.roll(xh, n - 1, 1), pltpu.roll(xh, 1, 1))
    return (xh * a + partner * b) * lax.rsqrt(ms + EPS)


def _pre_kernel(x_ref, mod_ref, w_ref, e_ref, aq_ref, bq_ref, ak_ref, bk_ref,
                qa_ref, ka_ref, va_ref, qb_ref, kb_ref, vb_ref):
    x = x_ref[0]
    u = _layer_norm(x) * (1.0 + mod_ref[0, 1:2, :]) + mod_ref[0, 0:1, :]
    ub = u.astype(BF16)
    o = 0
    qa = _dot(ub, w_ref[:, o:o + A_Q]); o += A_Q
    qa_ref[0] = _head_rms_rope(qa, e_ref[...], aq_ref[...], bq_ref[...]).astype(BF16)
    ka = _dot(ub, w_ref[:, o:o + A_KV]); o += A_KV
    ka_ref[0] = _head_rms_rope(ka, e_ref[:A_KV, :A_KV], ak_ref[...], bk_ref[...]).astype(BF16)
    va_ref[0] = _dot(ub, w_ref[:, o:o + A_KV]).astype(BF16); o += A_KV
    qb_ref[0] = (_dot(ub, w_ref[:, o:o + B_W]) * SCALE).astype(BF16); o += B_W
    kb_ref[0] = _dot(ub, w_ref[:, o:o + B_W]).astype(BF16); o += B_W
    vb_ref[0] = _dot(ub, w_ref[:, o:o + B_W]).astype(BF16)


def _pre_attention(x, mod, w_qkv, e, aq, bq, ak, bk, tile):
    bn, s, _ = x.shape
    tok = lambda width: pl.BlockSpec((1, tile, width), lambda b, i: (b, i, 0))
    tab = lambda width: pl.BlockSpec((tile, width), lambda b, i: (i, 0))
    out = lambda width: jax.ShapeDtypeStruct((bn, s, width), BF16)
    return pl.pallas_call(
        _pre_kernel,
        grid=(bn, s // tile),
        in_specs=[tok(D_MODEL),
                  pl.BlockSpec((1, 6, D_MODEL), lambda b, i: (b, 0, 0)),
                  _const_spec((D_MODEL, N_QKV)),
                  _const_spec((A_Q, A_Q)),
                  tab(A_Q), tab(A_Q), tab(A_KV), tab(A_KV)],
        out_specs=[tok(A_Q), tok(A_KV), tok(A_KV), tok(B_W), tok(B_W), tok(B_W)],
        out_shape=[out(A_Q), out(A_KV), out(A_KV), out(B_W), out(B_W), out(B_W)],
        compiler_params=_params(2),
        name="pre_attention",
    )(x, mod, w_qkv, e, aq, bq, ak, bk)


def _softmax_pv(s, v):
    m = jnp.max(s, -1, keepdims=True)
    p = jnp.exp(s - m)
    l = jnp.sum(p, -1, keepdims=True)
    return _dot(p.astype(BF16), v) / l


def _gqa_kernel(q_ref, k_ref, v_ref, o_ref):
    k = k_ref[0]
    v = v_ref[0]
    for c in range(A_Q // LANES):
        q = q_ref[0, :, c * LANES:(c + 1) * LANES]
        left = lax.broadcasted_iota(jnp.int32, q.shape, 1) < HEAD_DIM
        zero = jnp.zeros_like(q)
        o_left = _softmax_pv(_dot_nt(jnp.where(left, q, zero), k), v)
        o_right = _softmax_pv(_dot_nt(jnp.where(left, zero, q), k), v)
        o_ref[0, :, c * LANES:(c + 1) * LANES] = jnp.where(left, o_left, o_right).astype(BF16)


def _gqa_attention(qa, ka, va, tile):
    bn, s, _ = qa.shape
    return pl.pallas_call(
        _gqa_kernel,
        grid=(bn, s // tile),
        in_specs=[pl.BlockSpec((1, tile, A_Q), lambda b, i: (b, i, 0)),
                  pl.BlockSpec((1, s, A_KV), lambda b, i: (b, 0, 0)),
                  pl.BlockSpec((1, s, A_KV), lambda b, i: (b, 0, 0))],
        out_specs=pl.BlockSpec((1, tile, A_Q), lambda b, i: (b, i, 0)),
        out_shape=jax.ShapeDtypeStruct((bn, s, A_Q), BF16),
        compiler_params=_params(2),
        name="gqa_attention",
    )(qa, ka, va)


def _na_key_start(rb, n_rows):
    return jnp.clip(rb * NA_Q_ROWS - NA_ROWS // 2, 0, n_rows - NA_K_ROWS)


def _na_kernel(q_ref, k_ref, v_ref, bias_ref, o_ref, *, n_rows):
    start = pl.multiple_of(_na_key_start(pl.program_id(1), n_rows) * GRID_W, GRID_W)
    for c in range(B_W // LANES):
        lanes = slice(c * LANES, (c + 1) * LANES)
        q = q_ref[0, :, lanes]
        k = k_ref[0, pl.ds(start, NA_K), lanes]
        v = v_ref[0, pl.ds(start, NA_K), lanes]
        left = lax.broadcasted_iota(jnp.int32, q.shape, 1) < HEAD_DIM
        zero = jnp.zeros_like(q)
        o_left = _softmax_pv(_dot_nt(jnp.where(left, q, zero), k) + bias_ref[0, 2 * c], v)
        o_right = _softmax_pv(_dot_nt(jnp.where(left, zero, q), k) + bias_ref[0, 2 * c + 1], v)
        o_ref[0, :, lanes] = jnp.where(left, o_left, o_right).astype(BF16)


def _na_bias_table(rpb):
    pad = jnp.pad(rpb, ((0, 0), (0, 0), (GRID_W, GRID_W)))
    cols = jnp.stack([pad[:, :, GRID_W + NA_COLS - 1 - c:][:, :, :GRID_W] for c in range(GRID_W)],
                     axis=2)
    c = np.arange(GRID_W)
    cs = np.clip(c - NA_COLS // 2, 0, GRID_W - NA_COLS)
    in_cols = (c[None, :] >= cs[:, None]) & (c[None, :] < cs[:, None] + NA_COLS)
    cols = jnp.where(in_cols[None, None], cols, MASKED)
    masked = jnp.full((B_HEADS, GRID_W, GRID_W), MASKED, F32)
    n_rows = 3 * NA_K_ROWS
    kinds = []
    for r0 in (0, NA_K_ROWS, n_rows - NA_Q_ROWS):
        ws = int(np.clip(r0 - NA_ROWS // 2, 0, n_rows - NA_K_ROWS))
        q_rows = []
        for r in range(r0, r0 + NA_Q_ROWS):
            rs = int(np.clip(r - NA_ROWS // 2, 0, n_rows - NA_ROWS))
            blocks = [cols[:, kr - r + NA_ROWS - 1] if rs <= kr < rs + NA_ROWS else masked
                      for kr in range(ws, ws + NA_K_ROWS)]
            q_rows.append(jnp.concatenate(blocks, axis=-1))
        kinds.append(jnp.concatenate(q_rows, axis=1))
    return jnp.stack(kinds, axis=0)


def _neighbourhood_attention(qb, kb, vb, bias):
    bn, s, _ = qb.shape
    n_rows = s // GRID_W
    assert n_rows >= NA_K_ROWS + NA_Q_ROWS and n_rows % NA_Q_ROWS == 0
    n_blk = n_rows // NA_Q_ROWS

    def kind(rb):
        return jnp.where(rb == 0, 0, jnp.where(rb == n_blk - 1, 2, 1))

    return pl.pallas_call(
        functools.partial(_na_kernel, n_rows=n_rows),
        grid=(bn, n_blk),
        in_specs=[pl.BlockSpec((1, NA_Q, B_W), lambda b, r: (b, r, 0)),
                  pl.BlockSpec((1, s, B_W), lambda b, r: (b, 0, 0)),
                  pl.BlockSpec((1, s, B_W), lambda b, r: (b, 0, 0)),
                  pl.BlockSpec((1, B_HEADS, NA_Q, NA_K), lambda b, r: (kind(r), 0, 0, 0))],
        out_specs=pl.BlockSpec((1, NA_Q, B_W), lambda b, r: (b, r, 0)),
        out_shape=jax.ShapeDtypeStruct((bn, s, B_W), BF16),
        compiler_params=_params(2),
        name="neighbourhood_attention",
    )(qb, kb, vb, bias)


def _mix_kernel(x_ref, mod_ref, oa_ref, ob_ref, wg_ref, woa_ref, wob_ref, wo_ref, g_ref, b_ref,
                y_ref):
    x = x_ref[0]
    u = _layer_norm(x) * (1.0 + mod_ref[0, 1:2, :]) + mod_ref[0, 0:1, :]
    ub = u.astype(BF16)
    gate_a = jax.nn.sigmoid(_dot(ub, wg_ref[:, :D_MODEL]))
    gate_b = jax.nn.sigmoid(_dot(ub, wg_ref[:, D_MODEL:]))
    merged = gate_a * _dot(oa_ref[0], woa_ref[...]) + gate_b * _dot(ob_ref[0], wob_ref[...])
    mixed = _dot(merged.astype(BF16), wo_ref[...])
    y = ALPHA * x + (1.0 + mod_ref[0, 2:3, :]) * mixed
    y_ref[0] = _layer_norm(y) * g_ref[...] + b_ref[...]


def _mix(x, mod, oa, ob, w_gate, w_out_a, w_out_b, w_out, ln_g, ln_b, tile):
    bn, s, _ = x.shape
    tok = lambda width: pl.BlockSpec((1, tile, width), lambda b, i: (b, i, 0))
    return pl.pallas_call(
        _mix_kernel,
        grid=(bn, s // tile),
        in_specs=[tok(D_MODEL),
                  pl.BlockSpec((1, 6, D_MODEL), lambda b, i: (b, 0, 0)),
                  tok(A_Q), tok(B_W),
                  _const_spec((D_MODEL, 2 * D_MODEL)),
                  _const_spec((A_Q, D_MODEL)),
                  _const_spec((B_W, D_MODEL)),
                  _const_spec((D_MODEL, D_MODEL)),
                  _const_spec((1, D_MODEL)), _const_spec((1, D_MODEL))],
        out_specs=tok(D_MODEL),
        out_shape=jax.ShapeDtypeStruct(x.shape, F32),
        compiler_params=_params(2),
        name="mix",
    )(x, mod, oa, ob, w_gate, w_out_a, w_out_b, w_out, ln_g, ln_b)


def _ffn_kernel(x_ref, mod_ref, w1_ref, w2_ref, g_ref, b_ref, y_ref):
    x = x_ref[0]
    u = _layer_norm(x) * (1.0 + mod_ref[0, 4:5, :]) + mod_ref[0, 3:4, :]
    h = jnp.square(jnp.maximum(_dot(u.astype(BF16), w1_ref[...]), 0.0))
    f = _dot(h.astype(BF16), w2_ref[...])
    y = ALPHA * x + (1.0 + mod_ref[0, 5:6, :]) * f
    y_ref[0] = _layer_norm(y) * g_ref[...] + b_ref[...]


def _ffn(x, mod, w1, w2, ln_g, ln_b, tile):
    bn, s, _ = x.shape
    tok = pl.BlockSpec((1, tile, D_MODEL), lambda b, i: (b, i, 0))
    return pl.pallas_call(
        _ffn_kernel,
        grid=(bn, s // tile),
        in_specs=[tok,
                  pl.BlockSpec((1, 6, D_MODEL), lambda b, i: (b, 0, 0)),
                  _const_spec((D_MODEL, D_FF)),
                  _const_spec((D_FF, D_MODEL)),
                  _const_spec((1, D_MODEL)), _const_spec((1, D_MODEL))],
        out_specs=tok,
        out_shape=jax.ShapeDtypeStruct(x.shape, F32),
        compiler_params=_params(2),
        name="ffn",
    )(x, mod, w1, w2, ln_g, ln_b)


def _rope_tables(n_tok, gain):
    t = np.arange(n_tok)
    n_pairs_axis = HEAD_DIM // 4
    inv_freq = 1.0 / (ROPE_THETA ** (np.arange(n_pairs_axis) * 2.0 / (HEAD_DIM // 2)))
    ang = np.concatenate([(t // GRID_W)[:, None] * inv_freq[None, :],
                          (t % GRID_W)[:, None] * inv_freq[None, :]], -1)
    cos = np.repeat(np.cos(ang), 2, axis=-1).astype(np.float32)
    sin = np.repeat(np.sin(ang), 2, axis=-1).astype(np.float32)
    sin[:, 0::2] *= -1.0
    partner_gain = gain.reshape(HEAD_DIM // 2, 2)[:, ::-1].reshape(HEAD_DIM)
    return jnp.asarray(cos) * gain[None, :], jnp.asarray(sin) * partner_gain[None, :]


def _head_order():
    order = []
    for j in range(A_GROUP):
        order += [j, A_GROUP + j]
    return np.concatenate([np.arange(h * HEAD_DIM, (h + 1) * HEAD_DIM) for h in order])


def _layer_params(l, n_toks, w_in, q_norm_a, k_norm_a, rpb_b, w_out_a, w_out_b, w_out,
                  ln_mix_g, ln_mix_b, w_ff1, w_ff2, ln_ff_g, ln_ff_b):
    perm = _head_order()
    w = w_in[l]
    w_qkv = jnp.concatenate([w[:, :A_Q][:, perm], w[:, A_Q:N_QKV]], axis=1).astype(BF16)
    rope = {}
    for n_tok in n_toks:
        aq, bq = _rope_tables(n_tok, q_norm_a[l])
        ak, bk = _rope_tables(n_tok, k_norm_a[l])
        rope[n_tok] = (jnp.tile(aq, (1, A_HEADS)) * SCALE, jnp.tile(bq, (1, A_HEADS)) * SCALE,
                       jnp.tile(ak, (1, A_KV_HEADS)), jnp.tile(bk, (1, A_KV_HEADS)))
    row = lambda v: v[l].reshape(1, D_MODEL)
    return dict(
        w_qkv=w_qkv,
        rope=rope,
        bias=_na_bias_table(rpb_b[l]),
        w_gate=w[:, N_QKV:].astype(BF16),
        w_out_a=w_out_a[l][perm, :].astype(BF16),
        w_out_b=w_out_b[l].astype(BF16),
        w_out=w_out[l].astype(BF16),
        ln_mix=(row(ln_mix_g), row(ln_mix_b)),
        w_ff1=w_ff1[l].astype(BF16),
        w_ff2=w_ff2[l].astype(BF16),
        ln_ff=(row(ln_ff_g), row(ln_ff_b)),
    )


def _head_mean_matrix():
    blk = np.arange(A_Q) // HEAD_DIM
    return jnp.asarray((blk[:, None] == blk[None, :]) / HEAD_DIM, BF16)


def _encoder_layer(x, mod, p, e):
    s = x.shape[1]
    qa, ka, va, qb, kb, vb = _pre_attention(x, mod, p["w_qkv"], e, *p["rope"][s], tile=512)
    oa = _gqa_attention(qa, ka, va, tile=256)
    ob = _neighbourhood_attention(qb, kb, vb, p["bias"])
    x = _mix(x, mod, oa, ob, p["w_gate"], p["w_out_a"], p["w_out_b"], p["w_out"], *p["ln_mix"],
             tile=512)
    return _ffn(x, mod, p["w_ff1"], p["w_ff2"], *p["ln_ff"], tile=512)


def kernel(x_prompt, x_sample, c_prompt, c_sample, w_ada, b_ada, w_in, q_norm_a, k_norm_a, rpb_b,
           w_out_a, w_out_b, w_out, ln_mix_g, ln_mix_b, w_ff1, w_ff2, ln_ff_g, ln_ff_b):
    n_prompt, n_sample = c_prompt.shape[0], c_sample.shape[0]
    c_all = jnp.concatenate([c_prompt, c_sample], axis=0)
    c_all = jnp.pad(c_all, ((0, -c_all.shape[0] % 16), (0, 0)))
    mod = _modulation(c_all, w_ada, b_ada).reshape(DEPTH, -1, 6, D_MODEL)
    e = _head_mean_matrix()
    n_toks = (x_prompt.shape[1], x_sample.shape[1])
    y_prompt, y_sample = x_prompt, x_sample
    for l in range(DEPTH):
        p = _layer_params(l, n_toks, w_in, q_norm_a, k_norm_a, rpb_b, w_out_a, w_out_b, w_out,
                          ln_mix_g, ln_mix_b, w_ff1, w_ff2, ln_ff_g, ln_ff_b)
        y_prompt = _encoder_layer(y_prompt, mod[l, :n_prompt], p, e)
        y_sample = _encoder_layer(y_sample, mod[l, n_prompt:n_prompt + n_sample], p, e)
    return (y_prompt, y_sample)
```
